```python
import jax
import jax.numpy as jnp
from jax import lax
import numpy as np

D_MODEL = 4096
BATCH = 1
SEQ = 16384
DEPTH = 2

GRID_W = 64
CTX_LEN = 256
MIX_W = D_MODEL
HEAD_DIM = 128
N_Q_HEADS = (MIX_W // 2) // HEAD_DIM
N_KV_HEADS = N_Q_HEADS // 4
Q_PER_KV = N_Q_HEADS // N_KV_HEADS
ATTN_W = N_Q_HEADS * HEAD_DIM
KV_W = N_KV_HEADS * HEAD_DIM
WINDOW = 128
ATTN_BLOCK = 128
AXIS_ROT = HEAD_DIM // 2
ROPE_BASE = 10000.0
CONV_W = MIX_W // 4
CONV_K = 3
FOURIER_W = MIX_W // 4
FOURIER_GROUPS = 4
FOURIER_GROUP_W = FOURIER_W // FOURIER_GROUPS
IN_W = ATTN_W + 2 * KV_W + 3 * CONV_W + FOURIER_W
SPLITS = [ATTN_W, ATTN_W + KV_W, ATTN_W + 2 * KV_W, ATTN_W + 2 * KV_W + CONV_W,
          ATTN_W + 2 * KV_W + 2 * CONV_W, ATTN_W + 2 * KV_W + 3 * CONV_W]
N_EXPERTS = 32
TOP_K = 4
D_EXPERT = 768
SWIGLU_LIMIT = 7.0
SWIGLU_ALPHA = 1.702
MOE_BLOCK = 128
N_MOD = 6
RMS_EPS = 1e-5
NEG_INF = -1e30

kernel_name = 'hybrid_swa_shortconv_fnet_moe_dit'


def rmsnorm(x, g):
    xf = x.astype(jnp.float32)
    y = xf * lax.rsqrt(jnp.mean(xf * xf, axis=-1, keepdims=True) + RMS_EPS)
    return (y * g.astype(jnp.float32)).astype(x.dtype)


def adaln_params(cond, w_ada, b_ada):
    m = jax.nn.silu(cond) @ w_ada + b_ada
    return m.reshape(cond.shape[0], N_MOD, D_MODEL)


def modulate(h, shift, scale):
    return h * (1.0 + scale[:, None, :]) + shift[:, None, :]


def axial_rope_angles(rows):
    row = jnp.repeat(jnp.arange(rows, dtype=jnp.float32), GRID_W)
    col = jnp.tile(jnp.arange(GRID_W, dtype=jnp.float32), rows)
    inv_freq = ROPE_BASE ** (-jnp.arange(0, AXIS_ROT, 2, dtype=jnp.float32) / AXIS_ROT)
    ang = jnp.concatenate([row[:, None] * inv_freq, col[:, None] * inv_freq], axis=-1)
    return jnp.cos(ang), jnp.sin(ang)


def apply_axial_rope(x, cos, sin):
    b, t, h, _ = x.shape
    half = AXIS_ROT // 2
    xf = x.astype(jnp.float32).reshape(b, t, h, 2, 2, half)
    x1, x2 = xf[..., 0, :], xf[..., 1, :]
    cs = cos.reshape(t, 1, 2, half)
    sn = sin.reshape(t, 1, 2, half)
    out = jnp.stack([x1 * cs - x2 * sn, x1 * sn + x2 * cs], axis=-2)
    return out.reshape(b, t, h, HEAD_DIM).astype(x.dtype)


def band_mask(n_blocks, n_tok):
    blk = jnp.arange(n_blocks)[:, None, None]
    qpos = blk * ATTN_BLOCK + jnp.arange(ATTN_BLOCK)[None, :, None]
    kpos = (blk - 1) * ATTN_BLOCK + jnp.arange(3 * ATTN_BLOCK)[None, None, :]
    return (jnp.abs(kpos - qpos) <= WINDOW) & (kpos >= 0) & (kpos < n_tok)


def sink_column(attn_sink, lead_shape, n_q):
    s = attn_sink.astype(jnp.float32).reshape(N_KV_HEADS, Q_PER_KV, 1, 1)
    return jnp.broadcast_to(s, lead_shape + (N_KV_HEADS, Q_PER_KV, n_q, 1))


def windowed_attention(q, k, v, kc, vc, attn_sink):
    b, t = q.shape[:2]
    nb = t // ATTN_BLOCK
    scale = HEAD_DIM ** -0.5
    qb = q.reshape(b, nb, ATTN_BLOCK, N_KV_HEADS, Q_PER_KV, HEAD_DIM)

    def band(a):
        ap = jnp.pad(a, ((0, 0), (ATTN_BLOCK, ATTN_BLOCK), (0, 0), (0, 0)))
        ap = ap.reshape(b, nb + 2, ATTN_BLOCK, N_KV_HEADS, HEAD_DIM)
        return jnp.concatenate([ap[:, :-2], ap[:, 1:-1], ap[:, 2:]], axis=2)

    kb, vb = band(k), band(v)
    s_loc = jnp.einsum('bnqhgd,bnkhd->bnhgqk', qb, kb).astype(jnp.float32) * scale
    mask = band_mask(nb, t)[None, :, None, None]
    s_loc = jnp.where(mask, s_loc, NEG_INF)
    s_ctx = jnp.einsum('bnqhgd,bchd->bnhgqc', qb, kc).astype(jnp.float32) * scale
    logits = jnp.concatenate([s_loc, s_ctx, sink_column(attn_sink, (b, nb), ATTN_BLOCK)], axis=-1)
    p = jax.nn.softmax(logits, axis=-1)
    n_loc = 3 * ATTN_BLOCK
    n_ctx = kc.shape[1]
    p_loc = p[..., :n_loc].astype(v.dtype)
    p_ctx = p[..., n_loc:n_loc + n_ctx].astype(v.dtype)
    o = (jnp.einsum('bnhgqk,bnkhd->bnqhgd', p_loc, vb)
         + jnp.einsum('bnhgqc,bchd->bnqhgd', p_ctx, vc))
    return o.reshape(b, t, ATTN_W)


def context_attention(qc, kc, vc, attn_sink):
    b, lc = qc.shape[:2]
    scale = HEAD_DIM ** -0.5
    qg = qc.reshape(b, lc, N_KV_HEADS, Q_PER_KV, HEAD_DIM)
    s = jnp.einsum('bqhgd,bkhd->bhgqk', qg, kc).astype(jnp.float32) * scale
    logits = jnp.concatenate([s, sink_column(attn_sink, (b,), lc)], axis=-1)
    p = jax.nn.softmax(logits, axis=-1)[..., :lc].astype(vc.dtype)
    o = jnp.einsum('bhgqk,bkhd->bqhgd', p, vc)
    return o.reshape(b, lc, ATTN_W)


def short_conv_gate(bg, cg, hv, conv_w):
    t = hv.shape[1]
    z = cg * hv
    zp = jnp.pad(z, ((0, 0), (CONV_K // 2, CONV_K // 2), (0, 0)))
    y = zp[:, 0:t] * conv_w[0]
    for j in range(1, CONV_K):
        y = y + zp[:, j:j + t] * conv_w[j]
    return bg * y


def fourier_mix(u):
    b, t, _ = u.shape
    ug = u.astype(jnp.float32).reshape(b, t, FOURIER_GROUPS, FOURIER_GROUP_W)
    f = jnp.fft.fftn(ug, axes=(1, 3), norm='ortho').real
    return f.reshape(b, t, FOURIER_W).astype(u.dtype)


def moe_ffn(h, w_router, b_router, w_gu, b_gu, w_dn, b_dn):
    n_tok, d = h.shape
    logits = h.astype(jnp.float32) @ w_router.astype(jnp.float32) + b_router.astype(jnp.float32)
    top_val, top_idx = lax.top_k(logits, TOP_K)
    gate_w = jax.nn.softmax(top_val, axis=-1)
    n_asg = n_tok * TOP_K
    flat_e = top_idx.reshape(-1).astype(jnp.int32)
    flat_tok = jnp.repeat(jnp.arange(n_tok, dtype=jnp.int32), TOP_K)
    flat_w = gate_w.reshape(-1)
    order = jnp.argsort(flat_e)
    e_sorted = flat_e[order]
    counts = jnp.bincount(flat_e, length=N_EXPERTS).astype(jnp.int32)
    padded = (counts + MOE_BLOCK - 1) // MOE_BLOCK * MOE_BLOCK
    cum_pad = jnp.cumsum(padded)
    pad_start = cum_pad - padded
    start = jnp.cumsum(counts) - counts
    dest = pad_start[e_sorted] + jnp.arange(n_asg, dtype=jnp.int32) - start[e_sorted]
    n_blocks = -(-n_asg // MOE_BLOCK) + N_EXPERTS
    n_slots = n_blocks * MOE_BLOCK
    slot_tok = jnp.full((n_slots,), n_tok, jnp.int32).at[dest].set(flat_tok[order])
    slot_w = jnp.zeros((n_slots,), jnp.float32).at[dest].set(flat_w[order])
    block_start = jnp.arange(n_blocks, dtype=jnp.int32) * MOE_BLOCK
    block_e = jnp.minimum(jnp.searchsorted(cum_pad, block_start, side='right'), N_EXPERTS - 1)
    h_pad = jnp.concatenate([h, jnp.zeros((1, d), h.dtype)], axis=0)

    def expert_block(args):
        tok, e, w = args
        xb = h_pad[tok]
        gu = xb @ w_gu[e] + b_gu[e]
        gate = jnp.minimum(gu[:, :D_EXPERT], SWIGLU_LIMIT)
        up = jnp.clip(gu[:, D_EXPERT:], -SWIGLU_LIMIT, SWIGLU_LIMIT)
        act = (up + 1.0) * (gate * jax.nn.sigmoid(SWIGLU_ALPHA * gate))
        yb = act @ w_dn[e] + b_dn[e]
        return yb.astype(jnp.float32) * w[:, None]

    ys = lax.map(expert_block, (slot_tok.reshape(n_blocks, MOE_BLOCK), block_e,
                                slot_w.reshape(n_blocks, MOE_BLOCK)))
    out = jax.ops.segment_sum(ys.reshape(n_slots, d), slot_tok, num_segments=n_tok + 1)[:n_tok]
    return out.astype(h.dtype)


def hybrid_layer(x, xc, c, c_ctx, w_ada, b_ada, g_mix, w_in, conv_w, attn_sink, w_out,
                 g_ffn, w_router, b_router, w_gu, b_gu, w_dn, b_dn, rope_cos, rope_sin, last):
    b, t, d = x.shape
    lc = xc.shape[1]
    mod = adaln_params(c, w_ada, b_ada)
    mod_c = adaln_params(c_ctx[None], w_ada, b_ada)
    h = modulate(rmsnorm(x, g_mix), mod[:, 0], mod[:, 1])
    hc = modulate(rmsnorm(xc, g_mix), mod_c[:, 0], mod_c[:, 1])

    q, k, v, bg, cg, hv, fu = jnp.split(h @ w_in, SPLITS, axis=-1)
    q = apply_axial_rope(q.reshape(b, t, N_Q_HEADS, HEAD_DIM), rope_cos, rope_sin)
    k = apply_axial_rope(k.reshape(b, t, N_KV_HEADS, HEAD_DIM), rope_cos, rope_sin)
    v = v.reshape(b, t, N_KV_HEADS, HEAD_DIM)

    if last:
        kvc = hc @ w_in[:, ATTN_W:ATTN_W + 2 * KV_W]
        kc, vc = jnp.split(kvc, [KV_W], axis=-1)
    else:
        qc, kc, vc, bgc, cgc, hvc, fuc = jnp.split(hc @ w_in, SPLITS, axis=-1)
    kc = kc.reshape(b, lc, N_KV_HEADS, HEAD_DIM)
    vc = vc.reshape(b, lc, N_KV_HEADS, HEAD_DIM)

    mixed = jnp.concatenate([windowed_attention(q, k, v, kc, vc, attn_sink),
                             short_conv_gate(bg, cg, hv, conv_w),
                             fourier_mix(fu)], axis=-1) @ w_out
    x = x + mod[:, 2][:, None, :] * mixed
    h2 = modulate(rmsnorm(x, g_ffn), mod[:, 3], mod[:, 4])

    if last:
        y = moe_ffn(h2.reshape(b * t, d), w_router, b_router, w_gu, b_gu, w_dn, b_dn)
        x = x + mod[:, 5][:, None, :] * y.reshape(b, t, d)
        return x, None

    mixed_c = jnp.concatenate([context_attention(qc.reshape(b, lc, N_Q_HEADS, HEAD_DIM), kc, vc, attn_sink),
                               short_conv_gate(bgc, cgc, hvc, conv_w),
                               fourier_mix(fuc)], axis=-1) @ w_out
    xc = xc + mod_c[:, 2][:, None, :] * mixed_c
    h2c = modulate(rmsnorm(xc, g_ffn), mod_c[:, 3], mod_c[:, 4])
    tokens = jnp.concatenate([h2.reshape(b * t, d), h2c.reshape(b * lc, d)], axis=0)
    y = moe_ffn(tokens, w_router, b_router, w_gu, b_gu, w_dn, b_dn)
    x = x + mod[:, 5][:, None, :] * y[:b * t].reshape(b, t, d)
    xc = xc + mod_c[:, 5][:, None, :] * y[b * t:].reshape(b, lc, d)
    return x, xc


def setup_inputs(seed: int = 0) -> dict:
    key = jax.random.key(seed)
    ks = jax.random.split(key, 20)

    def nrm(k, shape, s):
        return jax.random.normal(k, shape, jnp.float32) * s

    L = DEPTH
    return {
        'x': nrm(ks[0], (BATCH, SEQ, D_MODEL), 1.0),
        'c': nrm(ks[1], (BATCH, D_MODEL), 1.0),
        'ctx': nrm(ks[2], (BATCH, CTX_LEN, D_MODEL), 1.0),
        'c_ctx': nrm(ks[3], (D_MODEL,), 1.0),
        'w_ada': nrm(ks[4], (L, D_MODEL, N_MOD * D_MODEL), 0.5 * D_MODEL ** -0.5),
        'b_ada': nrm(ks[5], (L, N_MOD * D_MODEL), 0.02),
        'g_mix': 1.0 + nrm(ks[6], (L, D_MODEL), 0.05),
        'w_in': nrm(ks[7], (L, D_MODEL, IN_W), D_MODEL ** -0.5),
        'conv_w': nrm(ks[8], (L, CONV_K, CONV_W), CONV_K ** -0.5),
        'attn_sink': nrm(ks[9], (L, N_Q_HEADS), 0.5),
        'w_out': nrm(ks[10], (L, MIX_W, D_MODEL), MIX_W ** -0.5),
        'g_ffn': 1.0 + nrm(ks[11], (L, D_MODEL), 0.05),
        'w_router': nrm(ks[12], (L, D_MODEL, N_EXPERTS), D_MODEL ** -0.5),
        'b_router': nrm(ks[13], (L, N_EXPERTS), 0.01),
        'w_gu': nrm(ks[14], (L, N_EXPERTS, D_MODEL, 2 * D_EXPERT), D_MODEL ** -0.5),
        'b_gu': nrm(ks[15], (L, N_EXPERTS, 2 * D_EXPERT), 0.02),
        'w_dn': nrm(ks[16], (L, N_EXPERTS, D_EXPERT, D_MODEL), D_EXPERT ** -0.5),
        'b_dn': nrm(ks[17], (L, N_EXPERTS, D_MODEL), 0.02),
        'g_final': 1.0 + nrm(ks[18], (D_MODEL,), 0.05),
    }


def reference(x, c, ctx, c_ctx, w_ada, b_ada, g_mix, w_in, conv_w, attn_sink, w_out, g_ffn,
              w_router, b_router, w_gu, b_gu, w_dn, b_dn, g_final):
    rows = x.shape[1] // GRID_W
    rope_cos, rope_sin = axial_rope_angles(rows)
    xc = ctx
    for l in range(DEPTH):
        x, xc = hybrid_layer(x, xc, c, c_ctx, w_ada[l], b_ada[l], g_mix[l], w_in[l], conv_w[l],
                             attn_sink[l], w_out[l], g_ffn[l], w_router[l], b_router[l],
                             w_gu[l], b_gu[l], w_dn[l], b_dn[l], rope_cos, rope_sin,
                             l == DEPTH - 1)
    return rmsnorm(x, g_final)
```

```python
import functools

import jax
import jax.numpy as jnp
import numpy as np
from jax import lax
from jax.experimental import pallas as pl
from jax.experimental.pallas import tpu as pltpu

F32 = jnp.float32
BF16 = jnp.bfloat16
I32 = jnp.int32

HEAD_DIM = 128
Q_PER_KV = 4
GRID_W = 64
ATTN_BLOCK = 128
AXIS_ROT = HEAD_DIM // 2
ROPE_BASE = 10000.0
CONV_K = 3
FOURIER_GROUPS = 4
TOP_K = 4
SWIGLU_LIMIT = 7.0
SWIGLU_ALPHA = 1.702
N_MOD = 6
RMS_EPS = 1e-5
NEG_INF = -1e30

V7X_VMEM_BYTES = 64 * 1024 * 1024
VMEM_CAP_BYTES = 56 * 1024 * 1024
MOD_ROWS = 8
MOE_TILE = 256
FOURIER_T2 = 128


def _cparams(sem, vmem_bytes):
    limit = int(min(VMEM_CAP_BYTES, max(vmem_bytes * 5 // 4 + (2 << 20), 16 << 20)))
    return pltpu.CompilerParams(dimension_semantics=sem, vmem_limit_bytes=limit)


def _pick(n, cands):
    for c in cands:
        if n % c == 0:
            return c
    raise ValueError(f"no tile in {cands} divides {n}")


def _adaln_kernel(c_ref, w_ref, b_ref, o_ref):
    c = c_ref[...]
    s = (c / (1.0 + jnp.exp(-c))).astype(BF16)
    o_ref[...] = jnp.dot(s, w_ref[...].astype(BF16), preferred_element_type=F32) + b_ref[...]


def adaln(c_rows, w_ada, b_ada):
    d, n = w_ada.shape
    tn = _pick(n, (512, 256, 128))
    vmem = 2 * (d * tn * 4) + d * tn * 2 + 4 * MOD_ROWS * (d + tn) * 4
    return pl.pallas_call(
        _adaln_kernel,
        grid=(n // tn,),
        in_specs=[pl.BlockSpec((MOD_ROWS, d), lambda j: (0, 0)),
                  pl.BlockSpec((d, tn), lambda j: (0, j)),
                  pl.BlockSpec((1, tn), lambda j: (0, j))],
        out_specs=pl.BlockSpec((MOD_ROWS, tn), lambda j: (0, j)),
        out_shape=jax.ShapeDtypeStruct((MOD_ROWS, n), F32),
        compiler_params=_cparams(("arbitrary",), vmem),
        name="adaln",
    )(c_rows, w_ada, b_ada.reshape(1, n))


def _norm_mod_kernel(x_ref, g_ref, shift_ref, scale_ref, o_ref, *, mod_row):
    x = x_ref[...]
    ms = jnp.mean(x * x, axis=-1, keepdims=True)
    y = x * lax.rsqrt(ms + RMS_EPS) * g_ref[...]
    sh = shift_ref[mod_row:mod_row + 1, :]
    sc = scale_ref[mod_row:mod_row + 1, :]
    o_ref[...] = (y * (1.0 + sc) + sh).astype(o_ref.dtype)


def norm_mod(x, g, mod_t, m_shift, m_scale, mod_row, out_dtype):
    r, d = x.shape
    tm = _pick(r, (256, 128))
    vmem = 2 * tm * d * 4 + 2 * tm * d * 4 + 6 * MOD_ROWS * d * 4
    return pl.pallas_call(
        functools.partial(_norm_mod_kernel, mod_row=mod_row),
        grid=(r // tm,),
        in_specs=[pl.BlockSpec((tm, d), lambda i: (i, 0)),
                  pl.BlockSpec((1, d), lambda i: (0, 0)),
                  pl.BlockSpec((None, MOD_ROWS, d), lambda i: (m_shift, 0, 0)),
                  pl.BlockSpec((None, MOD_ROWS, d), lambda i: (m_scale, 0, 0))],
        out_specs=pl.BlockSpec((tm, d), lambda i: (i, 0)),
        out_shape=jax.ShapeDtypeStruct((r, d), out_dtype),
        compiler_params=_cparams(("parallel",), vmem),
        name="norm_mod",
    )(x, g.reshape(1, d), mod_t, mod_t)


def _inproj_kernel(a_ref, w_ref, cos_ref, sin_ref, o_ref, *, n_rope_tiles, n_q_tiles, heads_per_tile,
                   q_scale, rope):
    j = pl.program_id(1)
    acc = jnp.dot(a_ref[...], w_ref[...], preferred_element_type=F32)

    @pl.when(j >= n_rope_tiles)
    def _():
        o_ref[...] = acc.astype(o_ref.dtype)

    @pl.when(j < n_rope_tiles)
    def _():
        scale = jnp.where(j < n_q_tiles, q_scale, 1.0).astype(F32)
        if rope:
            cos = cos_ref[...]
            sin = sin_ref[...]
            lane = lax.broadcasted_iota(I32, cos.shape, 1)
            first = (lane % AXIS_ROT) < (AXIS_ROT // 2)
        for hh in range(heads_per_tile):
            a = acc[:, hh * HEAD_DIM:(hh + 1) * HEAD_DIM]
            if rope:
                partner = jnp.where(first, pltpu.roll(a, HEAD_DIM - AXIS_ROT // 2, 1),
                                    pltpu.roll(a, AXIS_ROT // 2, 1))
                a = a * cos + partner * sin
            o_ref[:, hh * HEAD_DIM:(hh + 1) * HEAD_DIM] = (a * scale).astype(o_ref.dtype)


def in_proj(h, w_in_bf, cos_t, sin_t, attn_w, kv_w, rope):
    r, d = h.shape
    n = w_in_bf.shape[1]
    rope_cols = attn_w + kv_w
    tn = _pick(np.gcd(n, np.gcd(rope_cols, attn_w)), (512, 256, 128))
    tm = _pick(r, (1024, 512, 256, 128))
    vmem = 2 * tm * d * 2 + 2 * d * tn * 2 + 2 * tm * tn * 2 + 2 * tm * tn * 4 + 4 * tm * HEAD_DIM * 4
    kern = functools.partial(_inproj_kernel, n_rope_tiles=rope_cols // tn, n_q_tiles=attn_w // tn,
                             heads_per_tile=tn // HEAD_DIM, q_scale=HEAD_DIM ** -0.5, rope=rope)
    return pl.pallas_call(
        kern,
        grid=(r // tm, n // tn),
        in_specs=[pl.BlockSpec((tm, d), lambda i, j: (i, 0)),
                  pl.BlockSpec((d, tn), lambda i, j: (0, j)),
                  pl.BlockSpec((tm, HEAD_DIM), lambda i, j: (i, 0)),
                  pl.BlockSpec((tm, HEAD_DIM), lambda i, j: (i, 0))],
        out_specs=pl.BlockSpec((tm, tn), lambda i, j: (i, j)),
        out_shape=jax.ShapeDtypeStruct((r, n), BF16),
        compiler_params=_cparams(("parallel", "arbitrary"), vmem),
        name="in_proj",
    )(h, w_in_bf, cos_t, sin_t)


def _attn_kernel(*refs, n_kv, local, n_blocks):
    if local:
        sink_ref, q_ref, kp_ref, kc_ref, kn_ref, vp_ref, vc_ref, vn_ref, kx_ref, vx_ref, o_ref = refs
    else:
        sink_ref, q_ref, kx_ref, vx_ref, o_ref = refs
    n = pl.program_id(0)
    rows = Q_PER_KV * ATTN_BLOCK
    grp = lax.broadcasted_iota(I32, (rows, 1), 0) // ATTN_BLOCK
    nt = (((1,), (1,)), ((), ()))
    if local:
        shape = (rows, 3 * ATTN_BLOCK)
        qi = lax.broadcasted_iota(I32, shape, 0) % ATTN_BLOCK
        kk = lax.broadcasted_iota(I32, shape, 1)
        dist = kk - ATTN_BLOCK - qi
        lo = jnp.where(n == 0, ATTN_BLOCK, 0)
        hi = jnp.where(n == n_blocks - 1, 2 * ATTN_BLOCK, 3 * ATTN_BLOCK)
        bias = jnp.where(dist >= -ATTN_BLOCK, 0.0, 1.0)
        bias = jnp.where(dist <= ATTN_BLOCK, bias, 1.0)
        bias = jnp.where(kk >= lo, bias, 1.0)
        bias = jnp.where(kk < hi, bias, 1.0)
        valid = bias == 0.0
    for h in range(n_kv):
        hs = slice(h * HEAD_DIM, (h + 1) * HEAD_DIM)
        q = jnp.concatenate(
            [q_ref[:, (Q_PER_KV * h + g) * HEAD_DIM:(Q_PER_KV * h + g + 1) * HEAD_DIM] for g in range(Q_PER_KV)],
            axis=0)
        sink = jnp.full((rows, 1), sink_ref[Q_PER_KV * h + Q_PER_KV - 1], F32)
        for g in range(Q_PER_KV - 2, -1, -1):
            sink = jnp.where(grp == g, sink_ref[Q_PER_KV * h + g], sink)
        s_ctx = lax.dot_general(q, kx_ref[:, hs], nt, preferred_element_type=F32)
        m = jnp.maximum(jnp.max(s_ctx, axis=1, keepdims=True), sink)
        if local:
            kl = jnp.concatenate([kp_ref[:, hs], kc_ref[:, hs], kn_ref[:, hs]], axis=0)
            vl = jnp.concatenate([vp_ref[:, hs], vc_ref[:, hs], vn_ref[:, hs]], axis=0)
            s_loc = lax.dot_general(q, kl, nt, preferred_element_type=F32)
            s_loc = jnp.where(valid, s_loc, NEG_INF)
            m = jnp.maximum(m, jnp.max(s_loc, axis=1, keepdims=True))
        p_ctx = jnp.exp(s_ctx - m)
        denom = jnp.sum(p_ctx, axis=1, keepdims=True) + jnp.exp(sink - m)
        o = jnp.dot(p_ctx.astype(BF16), vx_ref[:, hs], preferred_element_type=F32)
        if local:
            p_loc = jnp.exp(s_loc - m)
            denom = denom + jnp.sum(p_loc, axis=1, keepdims=True)
            o = o + jnp.dot(p_loc.astype(BF16), vl, preferred_element_type=F32)
        o = o * (1.0 / denom)
        for g in range(Q_PER_KV):
            c0 = (Q_PER_KV * h + g) * HEAD_DIM
            o_ref[:, c0:c0 + HEAD_DIM] = o[g * ATTN_BLOCK:(g + 1) * ATTN_BLOCK].astype(o_ref.dtype)


def attention(hm_q, hm_ctx, sink, attn_w, kv_w, local):
    rq = hm_q.shape[0]
    lc = hm_ctx.shape[0]
    nb = rq // ATTN_BLOCK
    n_kv = kv_w // HEAD_DIM
    kcol = attn_w // kv_w
    vcol = kcol + 1
    blk = ATTN_BLOCK
    in_specs = [pl.BlockSpec(memory_space=pltpu.SMEM),
                pl.BlockSpec((blk, attn_w), lambda n: (n, 0))]
    args = [sink, hm_q]
    if local:
        for col in (kcol, vcol):
            in_specs += [pl.BlockSpec((blk, kv_w), lambda n, col=col: (jnp.maximum(n - 1, 0), col)),
                         pl.BlockSpec((blk, kv_w), lambda n, col=col: (n, col)),
                         pl.BlockSpec((blk, kv_w), lambda n, col=col: (jnp.minimum(n + 1, nb - 1), col))]
            args += [hm_q, hm_q, hm_q]
    in_specs += [pl.BlockSpec((lc, kv_w), lambda n: (0, kcol)),
                 pl.BlockSpec((lc, kv_w), lambda n: (0, vcol))]
    args += [hm_ctx, hm_ctx]
    vmem = 2 * blk * attn_w * 2 * 2 + 12 * blk * kv_w * 2 + 4 * lc * kv_w * 2 + 16 * (4 * blk) * (3 * blk + lc) * 4
    return pl.pallas_call(
        functools.partial(_attn_kernel, n_kv=n_kv, local=local, n_blocks=nb),
        grid=(nb,),
        in_specs=in_specs,
        out_specs=pl.BlockSpec((blk, attn_w), lambda n: (n, 0)),
        out_shape=jax.ShapeDtypeStruct((rq, attn_w), BF16),
        compiler_params=_cparams(("parallel",), vmem),
        name="attention_local" if local else "attention_ctx",
    )(*args)


HALO = 16


def _conv_kernel(bg_ref, cg_ref, hv_ref, cgp_ref, hvp_ref, cgn_ref, hvn_ref, w_ref, o_ref, z_scr, *, tm):
    i = pl.program_id(0)
    n = pl.num_programs(0)
    z = cg_ref[...].astype(F32) * hv_ref[...].astype(F32)
    zp = cgp_ref[...].astype(F32) * hvp_ref[...].astype(F32)
    zn = cgn_ref[...].astype(F32) * hvn_ref[...].astype(F32)
    zp_row = jnp.where(i > 0, zp[HALO - 1:HALO, :], 0.0)
    zn_row = jnp.where(i < n - 1, zn[0:1, :], 0.0)
    z_scr[8:8 + tm, :] = z
    z_scr[7:8, :] = zp_row
    z_scr[8 + tm:9 + tm, :] = zn_row
    w = w_ref[...]
    y = z_scr[7:7 + tm, :] * w[0:1, :] + z * w[1:2, :] + z_scr[9:9 + tm, :] * w[2:3, :]
    o_ref[...] = (bg_ref[...].astype(F32) * y).astype(o_ref.dtype)


def short_conv(hm, conv_w, col0):
    r = hm.shape[0]
    cw = conv_w.shape[1]
    tm = _pick(r, (256, 128))
    hb = tm // HALO
    nhb = r // HALO
    prev = lambda i, c: (jnp.maximum(i * hb - 1, 0), c)
    nxt = lambda i, c: (jnp.minimum((i + 1) * hb, nhb - 1), c)
    vmem = 2 * 4 * tm * cw * 2 + (tm + 16) * cw * 4 + 8 * tm * cw * 4
    return pl.pallas_call(
        functools.partial(_conv_kernel, tm=tm),
        grid=(r // tm,),
        in_specs=[pl.BlockSpec((tm, cw), lambda i: (i, col0)),
                  pl.BlockSpec((tm, cw), lambda i: (i, col0 + 1)),
                  pl.BlockSpec((tm, cw), lambda i: (i, col0 + 2)),
                  pl.BlockSpec((HALO, cw), lambda i: prev(i, col0 + 1)),
                  pl.BlockSpec((HALO, cw), lambda i: prev(i, col0 + 2)),
                  pl.BlockSpec((HALO, cw), lambda i: nxt(i, col0 + 1)),
                  pl.BlockSpec((HALO, cw), lambda i: nxt(i, col0 + 2)),
                  pl.BlockSpec((CONV_K, cw), lambda i: (0, 0))],
        out_specs=pl.BlockSpec((tm, cw), lambda i: (i, 0)),
        out_shape=jax.ShapeDtypeStruct((r, cw), BF16),
        scratch_shapes=[pltpu.VMEM((tm + 16, cw), F32)],
        compiler_params=_cparams(("parallel",), vmem),
        name="short_conv",
    )(hm, hm, hm, hm, hm, hm, hm, conv_w)


def _dft1_kernel(w_ref, u_ref, o_ref):
    o_ref[...] = jnp.dot(w_ref[...], u_ref[...], preferred_element_type=F32).astype(o_ref.dtype)


def _dft2_kernel(yre_ref, yim_ref, m_ref, cs_ref, o_ref, *, t2, gw):
    y = jnp.concatenate([yre_ref[...], yim_ref[...]], axis=0)
    x = jnp.dot(m_ref[...], y, preferred_element_type=F32).astype(BF16)
    for g in range(FOURIER_GROUPS):
        gs = slice(g * gw, (g + 1) * gw)
        o = (jnp.dot(x[:t2, gs], cs_ref[:gw, :], preferred_element_type=F32)
             + jnp.dot(x[t2:, gs], cs_ref[gw:, :], preferred_element_type=F32))
        o_ref[:, gs] = o.astype(o_ref.dtype)


def _dft_small_kernel(u_ref, ct_ref, st_ref, cs_ref, o_ref, *, gw):
    u = u_ref[...]
    for g in range(FOURIER_GROUPS):
        gs = slice(g * gw, (g + 1) * gw)
        a_c = jnp.dot(u[:, gs], cs_ref[:gw, :], preferred_element_type=F32).astype(BF16)
        a_s = jnp.dot(u[:, gs], cs_ref[gw:, :], preferred_element_type=F32).astype(BF16)
        o = (jnp.dot(ct_ref[...], a_c, preferred_element_type=F32)
             - jnp.dot(st_ref[...], a_s, preferred_element_type=F32))
        o_ref[:, gs] = o.astype(o_ref.dtype)


def _angles(num, den):
    return (2.0 * np.pi / den) * (num % den).astype(F32)


def fourier_tables(t, gw):
    t2 = FOURIER_T2
    t1 = t // t2
    k1 = jnp.arange(t1, dtype=I32)
    a1 = _angles(k1[:, None] * k1[None, :], t1)
    w1 = jnp.concatenate([jnp.cos(a1), -jnp.sin(a1)], axis=0).astype(BF16)
    k2 = jnp.arange(t2, dtype=I32)
    kfull = k1[:, None, None] + t1 * k2[None, :, None]
    a2 = _angles(kfull * k2[None, None, :], t)
    c2, s2 = jnp.cos(a2), jnp.sin(a2)
    m2 = jnp.concatenate([jnp.concatenate([c2, s2], axis=2),
                          jnp.concatenate([-s2, c2], axis=2)], axis=1).astype(BF16)
    cs = channel_table(t, gw)
    return w1, m2, cs


def channel_table(t, gw):
    c = jnp.arange(gw, dtype=I32)
    a = _angles(c[:, None] * c[None, :], gw)
    norm = 1.0 / np.sqrt(float(t) * gw)
    return (jnp.concatenate([jnp.cos(a), jnp.sin(a)], axis=0) * norm).astype(BF16)


def fourier_mix(fu, tables):
    w1, m2, cs = tables
    t, fw = fu.shape
    gw = fw // FOURIER_GROUPS
    t2 = FOURIER_T2
    t1 = t // t2
    cols = t2 * fw
    u2d = fu.reshape(t1, cols)
    tn = _pick(cols, (8192, 4096, 2048, 1024))
    y = pl.pallas_call(
        _dft1_kernel,
        grid=(cols // tn,),
        in_specs=[pl.BlockSpec((2 * t1, t1), lambda j: (0, 0)),
                  pl.BlockSpec((t1, tn), lambda j: (0, j))],
        out_specs=pl.BlockSpec((2 * t1, tn), lambda j: (0, j)),
        out_shape=jax.ShapeDtypeStruct((2 * t1, cols), BF16),
        compiler_params=_cparams(("parallel",), 2 * 3 * t1 * tn * 2 + 2 * t1 * tn * 4),
        name="fourier_stage1",
    )(w1, u2d)
    y = y.reshape(2 * t1 * t2, fw)
    out2d = pl.pallas_call(
        functools.partial(_dft2_kernel, t2=t2, gw=gw),
        grid=(t1,),
        in_specs=[pl.BlockSpec((t2, fw), lambda k: (k, 0)),
                  pl.BlockSpec((t2, fw), lambda k: (t1 + k, 0)),
                  pl.BlockSpec((None, 2 * t2, 2 * t2), lambda k: (k, 0, 0)),
                  pl.BlockSpec((2 * gw, gw), lambda k: (0, 0))],
        out_specs=pl.BlockSpec((t2, fw), lambda k: (0, k)),
        out_shape=jax.ShapeDtypeStruct((t2, t1 * fw), BF16),
        compiler_params=_cparams(("parallel",), 8 * t2 * fw * 2 + 6 * t2 * fw * 4),
        name="fourier_stage2",
    )(y, y, m2, cs)
    return out2d.reshape(t, fw)


def fourier_mix_small(fu, cs):
    t, fw = fu.shape
    gw = fw // FOURIER_GROUPS
    k = jnp.arange(t, dtype=I32)
    a = _angles(k[:, None] * k[None, :], t)
    ct, st = jnp.cos(a).astype(BF16), jnp.sin(a).astype(BF16)
    return pl.pallas_call(
        functools.partial(_dft_small_kernel, gw=gw),
        out_shape=jax.ShapeDtypeStruct((t, fw), BF16),
        compiler_params=_cparams(None, 4 * t * fw * 4 + 4 * t * t * 2),
        name="fourier_small",
    )(fu, ct, st, cs)


def _outproj_kernel(a1_ref, a2_ref, a3_ref, w1_ref, w2_ref, w3_ref, x_ref, gate_ref, o_ref, *, mod_row):
    acc = jnp.dot(a1_ref[...], w1_ref[...], preferred_element_type=F32)
    acc += jnp.dot(a2_ref[...], w2_ref[...], preferred_element_type=F32)
    acc += jnp.dot(a3_ref[...], w3_ref[...], preferred_element_type=F32)
    o_ref[...] = x_ref[...] + gate_ref[mod_row:mod_row + 1, :] * acc


def out_proj(attn, conv, four, w_out_bf, x, mod_t, m_gate, mod_row):
    r, d = x.shape
    aw, cw, fw = attn.shape[1], conv.shape[1], four.shape[1]
    tm = _pick(r, (1024, 512, 256, 128))
    tn = _pick(d, (512, 256, 128))
    vmem = 2 * tm * (aw + cw + fw) * 2 + 2 * (aw + cw + fw) * tn * 2 + 6 * tm * tn * 4
    return pl.pallas_call(
        functools.partial(_outproj_kernel, mod_row=mod_row),
        grid=(r // tm, d // tn),
        in_specs=[pl.BlockSpec((tm, aw), lambda i, j: (i, 0)),
                  pl.BlockSpec((tm, cw), lambda i, j: (i, 0)),
                  pl.BlockSpec((tm, fw), lambda i, j: (i, 0)),
                  pl.BlockSpec((aw, tn), lambda i, j: (0, j)),
                  pl.BlockSpec((cw, tn), lambda i, j: (aw // cw, j)),
                  pl.BlockSpec((fw, tn), lambda i, j: ((aw + cw) // fw, j)),
                  pl.BlockSpec((tm, tn), lambda i, j: (i, j)),
                  pl.BlockSpec((None, MOD_ROWS, tn), lambda i, j: (m_gate, 0, j))],
        out_specs=pl.BlockSpec((tm, tn), lambda i, j: (i, j)),
        out_shape=jax.ShapeDtypeStruct((r, d), F32),
        compiler_params=_cparams(("parallel", "arbitrary"), vmem),
        name="out_proj",
    )(attn, conv, four, w_out_bf, w_out_bf, w_out_bf, x, mod_t)


def _router_kernel(x_ref, g_ref, shift_ref, scale_ref, wr_ref, br_ref, c0_ref,
                   h_ref, idx_ref, gate_ref, rank_ref, cnt_ref, carry, *, mod_row, n_exp):
    i = pl.program_id(0)

    @pl.when(i == 0)
    def _():
        carry[...] = c0_ref[...]

    x = x_ref[...]
    tm = x.shape[0]
    ms = jnp.mean(x * x, axis=-1, keepdims=True)
    y = x * lax.rsqrt(ms + RMS_EPS) * g_ref[...]
    h = y * (1.0 + scale_ref[mod_row:mod_row + 1, :]) + shift_ref[mod_row:mod_row + 1, :]
    h_ref[...] = h
    logits = lax.dot_general(wr_ref[...], h, (((1,), (1,)), ((), ())), preferred_element_type=F32,
                             precision=lax.Precision.HIGHEST) + br_ref[...]
    e_iota = lax.broadcasted_iota(I32, (n_exp, tm), 0)
    work = logits
    vals, idxs = [], []
    for _ in range(TOP_K):
        m = jnp.max(work, axis=0, keepdims=True)
        am = jnp.min(jnp.where(work == m, e_iota, n_exp), axis=0, keepdims=True)
        vals.append(m)
        idxs.append(am)
        work = jnp.where(e_iota == am, -jnp.inf, work)
    exps = [jnp.exp(v - vals[0]) for v in vals]
    inv = 1.0 / (exps[0] + exps[1] + exps[2] + exps[3])
    onehot = jnp.zeros((n_exp, tm), F32)
    for am in idxs:
        onehot = onehot + jnp.where(e_iota == am, 1.0, 0.0)
    upper = jnp.where(lax.broadcasted_iota(I32, (tm, tm), 0) < lax.broadcasted_iota(I32, (tm, tm), 1),
                      1.0, 0.0).astype(BF16)
    before = jnp.dot(onehot.astype(BF16), upper, preferred_element_type=F32) + carry[:, 0:1]
    for k in range(TOP_K):
        idx_ref[k:k + 1, :] = idxs[k]
        gate_ref[k:k + 1, :] = exps[k] * inv
        rank = jnp.sum(jnp.where(e_iota == idxs[k], before, 0.0), axis=0, keepdims=True)
        rank_ref[k:k + 1, :] = rank.astype(I32)
    carry[...] = carry[...] + jnp.sum(onehot, axis=1, keepdims=True)
    cnt_ref[...] = carry[...]


def ffn_norm_router(x, g, mod_t, mod_row, w_router_t, b_router, counts0):
    r, d = x.shape
    n_exp = w_router_t.shape[0]
    tm = _pick(r, (256, 128))
    vmem = 4 * tm * d * 4 + 2 * n_exp * d * 4 + 4 * MOD_ROWS * d * 4 + 8 * tm * tm * 4
    outs = pl.pallas_call(
        functools.partial(_router_kernel, mod_row=mod_row, n_exp=n_exp),
        grid=(r // tm,),
        in_specs=[pl.BlockSpec((tm, d), lambda i: (i, 0)),
                  pl.BlockSpec((1, d), lambda i: (0, 0)),
                  pl.BlockSpec((None, MOD_ROWS, d), lambda i: (3, 0, 0)),
                  pl.BlockSpec((None, MOD_ROWS, d), lambda i: (4, 0, 0)),
                  pl.BlockSpec((n_exp, d), lambda i: (0, 0)),
                  pl.BlockSpec((n_exp, 1), lambda i: (0, 0)),
                  pl.BlockSpec((n_exp, 128), lambda i: (0, 0))],
        out_specs=[pl.BlockSpec((tm, d), lambda i: (i, 0)),
                   pl.BlockSpec((TOP_K, tm), lambda i: (0, i)),
                   pl.BlockSpec((TOP_K, tm), lambda i: (0, i)),
                   pl.BlockSpec((TOP_K, tm), lambda i: (0, i)),
                   pl.BlockSpec((n_exp, 128), lambda i: (0, 0))],
        out_shape=[jax.ShapeDtypeStruct((r, d), F32),
                   jax.ShapeDtypeStruct((TOP_K, r), I32),
                   jax.ShapeDtypeStruct((TOP_K, r), F32),
                   jax.ShapeDtypeStruct((TOP_K, r), I32),
                   jax.ShapeDtypeStruct((n_exp, 128), F32)],
        scratch_shapes=[pltpu.VMEM((n_exp, 128), F32)],
        compiler_params=_cparams(("arbitrary",), vmem),
        name="ffn_norm_router",
    )(x, g.reshape(1, d), mod_t, mod_t, w_router_t, b_router.reshape(n_exp, 1), counts0)
    return outs


def _row_copy(src_ref, row, buf, slot, r, sem):
    return pltpu.make_async_copy(src_ref.at[pl.ds(row, 1), :], buf.at[slot, pl.ds(r, 1), :], sem.at[slot])


def _dispatch_kernel(tok_ref, used_ref, h_ref, hc_ref, o_ref, buf, sem, *, mb, n_lat, has_ctx):
    i = pl.program_id(0)
    used = used_ref[0]

    def issue(tile, slot):
        def body(r, _):
            tok = tok_ref[tile * mb + r]
            if has_ctx:
                @pl.when(tok < n_lat)
                def _():
                    _row_copy(h_ref, tok, buf, slot, r, sem).start()

                @pl.when(tok >= n_lat)
                def _():
                    _row_copy(hc_ref, tok - n_lat, buf, slot, r, sem).start()
            else:
                _row_copy(h_ref, tok, buf, slot, r, sem).start()
            return 0
        lax.fori_loop(0, mb, body, 0)

    def wait(slot):
        def body(r, _):
            _row_copy(h_ref, 0, buf, slot, r, sem).wait()
            return 0
        lax.fori_loop(0, mb, body, 0)

    @pl.when((i == 0) & (used > 0))
    def _():
        issue(0, 0)

    @pl.when(i + 1 < used)
    def _():
        issue(i + 1, (i + 1) % 2)

    @pl.when(i < used)
    def _():
        wait(i % 2)
        o_ref[...] = buf[i % 2].astype(o_ref.dtype)

    @pl.when(i >= used)
    def _():
        o_ref[...] = jnp.zeros(o_ref.shape, o_ref.dtype)


def moe_dispatch(slot_tok, n_used, h2, h2c, n_tiles):
    n_lat, d = h2.shape
    has_ctx = h2c is not None
    mb = MOE_TILE
    vmem = 2 * mb * d * 4 + 2 * mb * d * 2 + mb * d * 4
    grid_spec = pltpu.PrefetchScalarGridSpec(
        num_scalar_prefetch=2,
        grid=(n_tiles,),
        in_specs=[pl.BlockSpec(memory_space=pl.ANY), pl.BlockSpec(memory_space=pl.ANY)],
        out_specs=pl.BlockSpec((mb, d), lambda i, tok, used: (i, 0)),
        scratch_shapes=[pltpu.VMEM((2, mb, d), F32), pltpu.SemaphoreType.DMA((2,))],
    )
    return pl.pallas_call(
        functools.partial(_dispatch_kernel, mb=mb, n_lat=n_lat, has_ctx=has_ctx),
        grid_spec=grid_spec,
        out_shape=jax.ShapeDtypeStruct((n_tiles * mb, d), BF16),
        compiler_params=_cparams(("arbitrary",), vmem),
        name="moe_dispatch",
    )(slot_tok, n_used, h2, h2c if has_ctx else h2)


def _expert_changed(be_ref, i):
    return (i == 0) | (be_ref[i] != be_ref[jnp.maximum(i - 1, 0)])


def _gate_up_kernel(be_ref, used_ref, x_ref, wg_ref, wu_ref, bg_ref, bu_ref, o_ref, wg_s, wu_s):
    i = pl.program_id(1)
    live = i < used_ref[0]

    @pl.when(live & _expert_changed(be_ref, i))
    def _():
        wg_s[...] = wg_ref[...].astype(BF16)
        wu_s[...] = wu_ref[...].astype(BF16)

    @pl.when(live)
    def _():
        x = x_ref[...]
        gate = jnp.dot(x, wg_s[...], preferred_element_type=F32) + bg_ref[...]
        up = jnp.dot(x, wu_s[...], preferred_element_type=F32) + bu_ref[...]
        gate = jnp.minimum(gate, SWIGLU_LIMIT)
        up = jnp.clip(up, -SWIGLU_LIMIT, SWIGLU_LIMIT)
        act = (up + 1.0) * (gate * (1.0 / (1.0 + jnp.exp(-SWIGLU_ALPHA * gate))))
        o_ref[...] = act.astype(o_ref.dtype)

    @pl.when(jnp.logical_not(live))
    def _():
        o_ref[...] = jnp.zeros(o_ref.shape, o_ref.dtype)


def _down_kernel(be_ref, used_ref, a_ref, w_ref, b_ref, o_ref, w_s):
    i = pl.program_id(1)
    live = i < used_ref[0]

    @pl.when(live & _expert_changed(be_ref, i))
    def _():
        w_s[...] = w_ref[...].astype(BF16)

    @pl.when(live)
    def _():
        o_ref[...] = jnp.dot(a_ref[...], w_s[...], preferred_element_type=F32) + b_ref[...]

    @pl.when(jnp.logical_not(live))
    def _():
        o_ref[...] = jnp.zeros(o_ref.shape, o_ref.dtype)


def moe_experts(x_sorted, block_e, n_used, w_gu, b_gu, w_dn, b_dn):
    n_slots, d = x_sorted.shape
    n_exp, _, f2 = w_gu.shape
    f = f2 // 2
    mb = MOE_TILE
    n_tiles = n_slots // mb
    tn = _pick(f, (384, 256, 128))
    nj = f // tn
    vmem = 2 * mb * d * 2 + 2 * 2 * d * tn * 4 + 2 * d * tn * 2 + 6 * mb * tn * 4
    act = pl.pallas_call(
        _gate_up_kernel,
        grid_spec=pltpu.PrefetchScalarGridSpec(
            num_scalar_prefetch=2,
            grid=(nj, n_tiles),
            in_specs=[pl.BlockSpec((mb, d), lambda j, i, be, used: (i, 0)),
                      pl.BlockSpec((None, d, tn), lambda j, i, be, used: (be[i], 0, j)),
                      pl.BlockSpec((None, d, tn), lambda j, i, be, used: (be[i], 0, nj + j)),
                      pl.BlockSpec((None, 1, tn), lambda j, i, be, used: (be[i], 0, j)),
                      pl.BlockSpec((None, 1, tn), lambda j, i, be, used: (be[i], 0, nj + j))],
            out_specs=pl.BlockSpec((mb, tn), lambda j, i, be, used: (i, j)),
            scratch_shapes=[pltpu.VMEM((d, tn), BF16), pltpu.VMEM((d, tn), BF16)],
        ),
        out_shape=jax.ShapeDtypeStruct((n_slots, f), BF16),
        compiler_params=_cparams(("arbitrary", "arbitrary"), vmem),
        name="moe_gate_up",
    )(block_e, n_used, x_sorted, w_gu, w_gu, b_gu.reshape(n_exp, 1, f2), b_gu.reshape(n_exp, 1, f2))

    tn2 = _pick(d, (2048, 1024, 512, 256, 128))
    vmem2 = 2 * mb * f * 2 + 2 * f * tn2 * 4 + f * tn2 * 2 + 4 * mb * tn2 * 4
    y = pl.pallas_call(
        _down_kernel,
        grid_spec=pltpu.PrefetchScalarGridSpec(
            num_scalar_prefetch=2,
            grid=(d // tn2, n_tiles),
            in_specs=[pl.BlockSpec((mb, f), lambda j, i, be, used: (i, 0)),
                      pl.BlockSpec((None, f, tn2), lambda j, i, be, used: (be[i], 0, j)),
                      pl.BlockSpec((None, 1, tn2), lambda j, i, be, used: (be[i], 0, j))],
            out_specs=pl.BlockSpec((mb, tn2), lambda j, i, be, used: (i, j)),
            scratch_shapes=[pltpu.VMEM((f, tn2), BF16)],
        ),
        out_shape=jax.ShapeDtypeStruct((n_slots, d), F32),
        compiler_params=_cparams(("arbitrary", "arbitrary"), vmem2),
        name="moe_down",
    )(block_e, n_used, act, w_dn, b_dn.reshape(n_exp, 1, d))
    return y


def _combine_kernel(dest_ref, y_ref, x_ref, w_ref, gate_ref, g_ref, o_ref, buf, sem, *,
                    tc, n_all, tok0, mod_row, final_norm):
    i = pl.program_id(0)
    n = pl.num_programs(0)

    def copy(row, slot, k, r):
        return pltpu.make_async_copy(y_ref.at[pl.ds(row, 1), :], buf.at[slot, k, pl.ds(r, 1), :], sem.at[slot])

    def issue(tile, slot):
        def body(r, _):
            for k in range(TOP_K):
                copy(dest_ref[k * n_all + tok0 + tile * tc + r], slot, k, r).start()
            return 0
        lax.fori_loop(0, tc, body, 0)

    def wait(slot):
        def body(r, _):
            for k in range(TOP_K):
                copy(0, slot, k, r).wait()
            return 0
        lax.fori_loop(0, tc, body, 0)

    @pl.when(i == 0)
    def _():
        issue(0, 0)

    @pl.when(i + 1 < n)
    def _():
        issue(i + 1, (i + 1) % 2)

    slot = i % 2
    wait(slot)
    w = w_ref[...]
    acc = buf[slot, 0] * w[:, 0:1]
    for k in range(1, TOP_K):
        acc = acc + buf[slot, k] * w[:, k:k + 1]
    xn = x_ref[...] + gate_ref[mod_row:mod_row + 1, :] * acc
    if final_norm:
        ms = jnp.mean(xn * xn, axis=-1, keepdims=True)
        xn = xn * lax.rsqrt(ms + RMS_EPS) * g_ref[...]
    o_ref[...] = xn


def moe_combine(dest_flat, y_sorted, x, gates_t, mod_t, mod_row, tok0, n_all, g_final):
    r, d = x.shape
    tc = 128
    final_norm = g_final is not None
    g = g_final.reshape(1, d) if final_norm else jnp.ones((1, d), F32)
    vmem = 2 * TOP_K * tc * d * 4 + 6 * tc * d * 4
    grid_spec = pltpu.PrefetchScalarGridSpec(
        num_scalar_prefetch=1,
        grid=(r // tc,),
        in_specs=[pl.BlockSpec(memory_space=pl.ANY),
                  pl.BlockSpec((tc, d), lambda i, dest: (i, 0)),
                  pl.BlockSpec((tc, TOP_K), lambda i, dest: (tok0 // tc + i, 0)),
                  pl.BlockSpec((None, MOD_ROWS, d), lambda i, dest: (5, 0, 0)),
                  pl.BlockSpec((1, d), lambda i, dest: (0, 0))],
        out_specs=pl.BlockSpec((tc, d), lambda i, dest: (i, 0)),
        scratch_shapes=[pltpu.VMEM((2, TOP_K, tc, d), F32), pltpu.SemaphoreType.DMA((2,))],
    )
    return pl.pallas_call(
        functools.partial(_combine_kernel, tc=tc, n_all=n_all, tok0=tok0, mod_row=mod_row,
                          final_norm=final_norm),
        grid_spec=grid_spec,
        out_shape=jax.ShapeDtypeStruct((r, d), F32),
        compiler_params=_cparams(("arbitrary",), vmem),
        name="moe_combine",
    )(dest_flat, y_sorted, x, gates_t, mod_t, g)


def moe_layer(xs, g_ffn, mod_t, w_router, b_router, w_gu, b_gu, w_dn, b_dn, g_final):
    n_exp = w_router.shape[1]
    w_router_t = w_router.T
    counts = jnp.zeros((n_exp, 128), F32)
    h2s, idxs, gates, ranks = [], [], [], []
    for x, mod_row in xs:
        h2, idx, gate, rank, counts = ffn_norm_router(x, g_ffn, mod_t, mod_row, w_router_t, b_router, counts)
        h2s.append(h2)
        idxs.append(idx)
        gates.append(gate)
        ranks.append(rank)
    idx = jnp.concatenate(idxs, axis=1)
    gate = jnp.concatenate(gates, axis=1)
    rank = jnp.concatenate(ranks, axis=1)
    n_all = idx.shape[1]
    mb = MOE_TILE
    n_tiles = (n_all * TOP_K) // mb + n_exp
    cnt = counts[:, 0].astype(I32)
    padded = (cnt + mb - 1) // mb * mb
    cum_pad = jnp.cumsum(padded)
    pad_start = cum_pad - padded
    dest = pad_start[idx] + rank
    tok = jnp.broadcast_to(jnp.arange(n_all, dtype=I32)[None, :], dest.shape)
    slot_tok = jnp.zeros((n_tiles * mb,), I32).at[dest.reshape(-1)].set(tok.reshape(-1))
    tile_start = jnp.arange(n_tiles, dtype=I32) * mb
    block_e = jnp.minimum(jnp.searchsorted(cum_pad, tile_start, side='right'), n_exp - 1).astype(I32)
    n_used = (cum_pad[-1] // mb).astype(I32).reshape(1)

    x_sorted = moe_dispatch(slot_tok, n_used, h2s[0], h2s[1] if len(h2s) > 1 else None, n_tiles)
    y_sorted = moe_experts(x_sorted, block_e, n_used, w_gu, b_gu, w_dn, b_dn)
    dest_flat = dest.reshape(-1)
    gates_t = gate.T
    outs = []
    tok0 = 0
    for x, mod_row in xs:
        outs.append(moe_combine(dest_flat, y_sorted, x, gates_t, mod_t, mod_row, tok0, n_all, g_final))
        tok0 += x.shape[0]
    return outs


def rope_tables(t):
    rows = t // GRID_W
    row = jnp.repeat(jnp.arange(rows, dtype=F32), GRID_W)
    col = jnp.tile(jnp.arange(GRID_W, dtype=F32), rows)
    inv_freq = ROPE_BASE ** (-jnp.arange(0, AXIS_ROT, 2, dtype=F32) / AXIS_ROT)
    ar = row[:, None] * inv_freq
    ac = col[:, None] * inv_freq
    ang = jnp.concatenate([ar, ar, ac, ac], axis=-1)
    sign = jnp.tile(jnp.concatenate([-jnp.ones((AXIS_ROT // 2,), F32), jnp.ones((AXIS_ROT // 2,), F32)]), 2)
    return jnp.cos(ang), jnp.sin(ang) * sign


def kernel(x, c, ctx, c_ctx, w_ada, b_ada, g_mix, w_in, conv_w, attn_sink, w_out, g_ffn, w_router, b_router,
           w_gu, b_gu, w_dn, b_dn, g_final):
    b, t, d = x.shape
    assert b == 1, "kernel handles a single sample"
    lc = ctx.shape[1]
    depth = w_ada.shape[0]
    in_w = w_in.shape[2]
    cw = conv_w.shape[2]
    attn_w = attn_sink.shape[1] * HEAD_DIM
    kv_w = attn_w // Q_PER_KV
    fw = in_w - attn_w - 2 * kv_w - 3 * cw
    gw = fw // FOURIER_GROUPS
    conv_col0 = (attn_w + 2 * kv_w) // cw
    four_col0 = attn_w + 2 * kv_w + 3 * cw

    xl = x[0]
    xc = ctx[0]
    c_rows = jnp.zeros((MOD_ROWS, d), F32).at[0].set(c[0]).at[1].set(c_ctx)
    cos_t, sin_t = rope_tables(t)
    ones_c = jnp.ones((lc, HEAD_DIM), F32)
    zeros_c = jnp.zeros((lc, HEAD_DIM), F32)
    tables = fourier_tables(t, gw)
    cs_ctx = channel_table(lc, gw)

    for l in range(depth):
        last = l == depth - 1
        mod = adaln(c_rows, w_ada[l], b_ada[l])
        mod_t = mod.reshape(MOD_ROWS, N_MOD, d).transpose(1, 0, 2)
        w_in_bf = w_in[l].astype(BF16)
        w_out_bf = w_out[l].astype(BF16)

        h = norm_mod(xl, g_mix[l], mod_t, 0, 1, 0, BF16)
        hc = norm_mod(xc, g_mix[l], mod_t, 0, 1, 1, BF16)
        hm = in_proj(h, w_in_bf, cos_t, sin_t, attn_w, kv_w, rope=True)
        hmc = in_proj(hc, w_in_bf, ones_c, zeros_c, attn_w, kv_w, rope=False)

        attn = attention(hm, hmc, attn_sink[l], attn_w, kv_w, local=True)
        conv = short_conv(hm, conv_w[l], conv_col0)
        four = fourier_mix(hm[:, four_col0:], tables)
        xl = out_proj(attn, conv, four, w_out_bf, xl, mod_t, 2, 0)

        if last:
            (xl,) = moe_layer([(xl, 0)], g_ffn[l], mod_t, w_router[l], b_router[l], w_gu[l], b_gu[l],
                              w_dn[l], b_dn[l], g_final)
        else:
            attn_c = attention(hmc, hmc, attn_sink[l], attn_w, kv_w, local=False)
            conv_c = short_conv(hmc, conv_w[l], conv_col0)
            four_c = fourier_mix_small(hmc[:, four_col0:], cs_ctx)
            xc = out_proj(attn_c, conv_c, four_c, w_out_bf, xc, mod_t, 2, 1)
            xl, xc = moe_layer([(xl, 0), (xc, 1)], g_ffn[l], mod_t, w_router[l], b_router[l], w_gu[l],
                               b_gu[l], w_dn[l], b_dn[l], None)
    return xl[None]
```
